```python
import math
import jax, jax.numpy as jnp
from jax import lax
import numpy as np

D_MODEL = 2048
BATCH = 8
SEQ = 4096
DEPTH = 4
DEC_BATCH = 2
DEC_SEQ = 4096
PAST_LEN = 128

GRID_W = 64
NA_HEAD_DIM = 64
NA_HEADS = D_MODEL // (2 * NA_HEAD_DIM)
NA_WIN_ROWS = 8
NA_WIN_COLS = 16
NA_QBLK_COLS = 16
SW_HEAD_DIM = 64
SW_Q_HEADS = D_MODEL // (2 * SW_HEAD_DIM)
SW_KV_HEADS = SW_Q_HEADS // 4
SW_RADIUS = 128
DIL_PAIRS = ((128, 1), (512, 4), (2048, 16))
DIL_HEAD_DIM = 64
DIL_HEADS = D_MODEL // (2 * DIL_HEAD_DIM)
ROPE_THETA = 500000.0
ROPE_FRACTION = 4
FFN_HIDDEN = ((8 * D_MODEL + 3 * 256 - 1) // (3 * 256)) * 256
RMS_EPS = 1e-6
NEG_INF = -1e30

A_W = NA_HEADS * NA_HEAD_DIM
B_QW = SW_Q_HEADS * SW_HEAD_DIM
B_KVW = SW_KV_HEADS * SW_HEAD_DIM
AB_IN = 3 * A_W + B_QW + 2 * B_KVW
AB_OUT = A_W + B_QW
C_W = DIL_HEADS * DIL_HEAD_DIM
C_IN = len(DIL_PAIRS) * 3 * C_W
C_OUT = C_W

kernel_name = "hybrid_natten_swa_dilated_encoder"


def rms_norm(x, g):
    xf = x.astype(jnp.float32)
    y = xf * lax.rsqrt(jnp.mean(xf * xf, axis=-1, keepdims=True) + RMS_EPS)
    return (y * g.astype(jnp.float32)).astype(x.dtype)


def rotary(x, pos):
    rot = x.shape[-1] // ROPE_FRACTION
    half = rot // 2
    inv = jnp.exp(-math.log(ROPE_THETA) * jnp.arange(half, dtype=jnp.float32) * (2.0 / rot))
    ang = pos[:, None] * inv[None, :]
    cos = jnp.cos(ang)[:, None, :].astype(x.dtype)
    sin = jnp.sin(ang)[:, None, :].astype(x.dtype)
    x1 = x[..., :half]
    x2 = x[..., half:rot]
    return jnp.concatenate([x1 * cos - x2 * sin, x2 * cos + x1 * sin, x[..., rot:]], axis=-1)


def banded_attention(q, k, v, radius, sink=None):
    n, L, hq, dh = q.shape
    hk = k.shape[2]
    g = hq // hk
    blk = radius
    nb = -(-L // blk)
    lp = nb * blk
    pad = lp - L
    qb = jnp.pad(q, ((0, 0), (0, pad), (0, 0), (0, 0))).reshape(n, nb, blk, hk, g, dh)

    def key_windows(t):
        tp = jnp.pad(t, ((0, 0), (blk, pad + blk), (0, 0), (0, 0))).reshape(n, nb + 2, blk, hk, dh)
        return jnp.concatenate([tp[:, :-2], tp[:, 1:-1], tp[:, 2:]], axis=2)

    kw, vw = key_windows(k), key_windows(v)
    qpos = jnp.arange(lp).reshape(nb, blk)
    kpos = (jnp.arange(nb)[:, None] - 1) * blk + jnp.arange(3 * blk)[None, :]
    kp = kpos[:, None, :]
    mask = (jnp.abs(qpos[:, :, None] - kp) <= radius) & (kp >= 0) & (kp < L)
    s = jnp.einsum('nbikgd,nbukd->nbkgiu', qb, kw).astype(jnp.float32) * (dh ** -0.5)
    s = jnp.where(mask[None, :, None, None], s, NEG_INF)
    m = jnp.max(s, axis=-1)
    if sink is not None:
        sk = sink.astype(jnp.float32).reshape(hk, g)[None, None, :, :, None]
        m = jnp.maximum(m, sk)
    p = jnp.exp(s - m[..., None])
    den = jnp.sum(p, axis=-1)
    if sink is not None:
        den = den + jnp.exp(sk - m)
    o = jnp.einsum('nbkgiu,nbukd->nbikgd', (p / den[..., None]).astype(v.dtype), vw)
    lse = jnp.moveaxis(m + jnp.log(den), -1, 2)
    o = o.reshape(n, lp, hq, dh)[:, :L]
    lse = lse.reshape(n, lp, hq)[:, :L]
    return o, lse


def dilated_attention(q, k, v, window, dilation):
    b, L, h, dh = q.shape
    radius = window // (2 * dilation)
    lm = L // dilation

    def split(t):
        return t.reshape(b, lm, dilation, h, dh).transpose(0, 2, 1, 3, 4).reshape(b * dilation, lm, h, dh)

    o, lse = banded_attention(split(q), split(k), split(v), radius)
    o = o.reshape(b, dilation, lm, h, dh).transpose(0, 2, 1, 3, 4).reshape(b, L, h, dh)
    lse = lse.reshape(b, dilation, lm, h).transpose(0, 2, 1, 3).reshape(b, L, h)
    return o, lse


def neighbourhood_attention(q, k, v, rpb):
    b, L, h, dh = q.shape
    rows = L // GRID_W
    kh = min(NA_WIN_ROWS, rows)
    kw = NA_WIN_COLS
    qr = math.gcd(rows, NA_WIN_ROWS)
    qc = NA_QBLK_COLS
    kr = min(kh + qr - 1, rows)
    kc = min(kw + qc - 1, GRID_W)
    nrb, ncb = rows // qr, GRID_W // qc
    r0 = jnp.arange(nrb) * qr
    c0 = jnp.arange(ncb) * qc
    krow = jnp.clip(r0 - kh // 2, 0, rows - kr)[:, None] + jnp.arange(kr)[None, :]
    kcol = jnp.clip(c0 - kw // 2, 0, GRID_W - kc)[:, None] + jnp.arange(kc)[None, :]
    qrow = r0[:, None] + jnp.arange(qr)[None, :]
    qcol = c0[:, None] + jnp.arange(qc)[None, :]
    rs = jnp.clip(qrow - kh // 2, 0, rows - kh)[:, :, None]
    cs = jnp.clip(qcol - kw // 2, 0, GRID_W - kw)[:, :, None]
    row_ok = (krow[:, None, :] >= rs) & (krow[:, None, :] < rs + kh)
    col_ok = (kcol[:, None, :] >= cs) & (kcol[:, None, :] < cs + kw)
    dr = jnp.clip(krow[:, None, :] - qrow[:, :, None] + NA_WIN_ROWS - 1, 0, 2 * NA_WIN_ROWS - 2)
    dc = jnp.clip(kcol[:, None, :] - qcol[:, :, None] + NA_WIN_COLS - 1, 0, 2 * NA_WIN_COLS - 2)
    bias = rpb.astype(jnp.float32)[:, dr[:, None, :, None, :, None], dc[None, :, None, :, None, :]]
    mask = row_ok[:, None, :, None, :, None] & col_ok[None, :, None, :, None, :]
    bias = jnp.moveaxis(jnp.where(mask[None], bias, NEG_INF), 0, 2)

    qg = q.reshape(b, nrb, qr, ncb, qc, h, dh)

    def gather(t):
        t = t.reshape(b, rows, GRID_W, h, dh)
        t = jnp.take(t, krow, axis=1)
        return jnp.take(t, kcol, axis=3)

    kg, vg = gather(k), gather(v)
    s = jnp.einsum('bRiCjhd,bRuCwhd->bRChijuw', qg, kg).astype(jnp.float32) * (dh ** -0.5) + bias[None]
    p = jax.nn.softmax(s.reshape(s.shape[:-2] + (kr * kc,)), axis=-1).reshape(s.shape)
    o = jnp.einsum('bRChijuw,bRuCwhd->bRiCjhd', p.astype(v.dtype), vg)
    return o.reshape(b, L, h, dh)


def ab_mixer(h, w_in, w_out, rpb, sink, pos):
    b, L, _ = h.shape
    proj = jnp.einsum('bld,de->ble', h, w_in)
    o3 = 3 * A_W
    o4 = o3 + B_QW
    o5 = o4 + B_KVW
    qa = proj[..., :A_W].reshape(b, L, NA_HEADS, NA_HEAD_DIM)
    ka = proj[..., A_W:2 * A_W].reshape(b, L, NA_HEADS, NA_HEAD_DIM)
    va = proj[..., 2 * A_W:o3].reshape(b, L, NA_HEADS, NA_HEAD_DIM)
    qb = rotary(proj[..., o3:o4].reshape(b, L, SW_Q_HEADS, SW_HEAD_DIM), pos)
    kb = rotary(proj[..., o4:o5].reshape(b, L, SW_KV_HEADS, SW_HEAD_DIM), pos)
    vb = proj[..., o5:].reshape(b, L, SW_KV_HEADS, SW_HEAD_DIM)
    out_a = neighbourhood_attention(qa, ka, va, rpb)
    out_b, _ = banded_attention(qb, kb, vb, SW_RADIUS, sink)
    cat = jnp.concatenate([out_a.reshape(b, L, A_W), out_b.reshape(b, L, B_QW)], axis=-1)
    return jnp.einsum('ble,ed->bld', cat, w_out)


def c_mixer(h, w_in, w_out, pos):
    b, L, _ = h.shape
    proj = jnp.einsum('bld,de->ble', h, w_in).reshape(b, L, len(DIL_PAIRS), 3, DIL_HEADS, DIL_HEAD_DIM)
    outs, lses = [], []
    for gi, (win, dil) in enumerate(DIL_PAIRS):
        q = rotary(proj[:, :, gi, 0], pos)
        k = rotary(proj[:, :, gi, 1], pos)
        o, lse = dilated_attention(q, k, proj[:, :, gi, 2], win, dil)
        outs.append(o)
        lses.append(lse)
    wts = jax.nn.softmax(jnp.stack(lses, axis=0), axis=0)
    o = jnp.sum(wts[..., None] * jnp.stack(outs, axis=0).astype(jnp.float32), axis=0)
    return jnp.einsum('ble,ed->bld', o.astype(h.dtype).reshape(b, L, C_W), w_out)


def swiglu(h, wg, wu, wd):
    hid = jax.nn.silu(jnp.einsum('bld,df->blf', h, wg)) * jnp.einsum('bld,df->blf', h, wu)
    return jnp.einsum('blf,fd->bld', hid, wd)


def trunk(x, g_mix_pre, g_mix_post, g_ffn_pre, g_ffn_post, w_in_ab, w_out_ab, rpb_a,
          sink_b, w_in_c, w_out_c, w_gate, w_up, w_down):
    L = x.shape[1]
    pos = jnp.arange(L, dtype=jnp.float32)
    for layer in range(DEPTH):
        i = layer // 2
        hn = rms_norm(x, g_mix_pre[layer])
        if layer % 2 == 0:
            m = ab_mixer(hn, w_in_ab[i], w_out_ab[i], rpb_a[i], sink_b[i], pos)
        else:
            m = c_mixer(hn, w_in_c[i], w_out_c[i], pos)
        x = x + rms_norm(m, g_mix_post[layer])
        f = swiglu(rms_norm(x, g_ffn_pre[layer]), w_gate[layer], w_up[layer], w_down[layer])
        x = x + rms_norm(f, g_ffn_post[layer])
    return x


def setup_inputs(seed: int = 0) -> dict:
    key = jax.random.key(seed)
    ks = jax.random.split(key, 16)
    n_even = (DEPTH + 1) // 2
    n_odd = DEPTH // 2

    def w(k, shape, fan_in):
        return jax.random.normal(k, shape, jnp.float32) * (fan_in ** -0.5)

    def gain(k):
        return 1.0 + 0.05 * jax.random.normal(k, (DEPTH, D_MODEL), jnp.float32)

    return {
        "x_prompt": jax.random.normal(ks[0], (BATCH, SEQ, D_MODEL), jnp.float32),
        "x_sample": jax.random.normal(ks[1], (DEC_BATCH, DEC_SEQ, D_MODEL), jnp.float32),
        "g_mix_pre": gain(ks[2]),
        "g_mix_post": gain(ks[3]),
        "g_ffn_pre": gain(ks[4]),
        "g_ffn_post": gain(ks[5]),
        "w_in_ab": w(ks[6], (n_even, D_MODEL, AB_IN), D_MODEL),
        "w_out_ab": w(ks[7], (n_even, AB_OUT, D_MODEL), AB_OUT),
        "rpb_a": 0.1 * jax.random.normal(ks[8], (n_even, NA_HEADS, 2 * NA_WIN_ROWS - 1, 2 * NA_WIN_COLS - 1), jnp.float32),
        "sink_b": jax.random.normal(ks[9], (n_even, SW_Q_HEADS), jnp.float32),
        "w_in_c": w(ks[10], (n_odd, D_MODEL, C_IN), D_MODEL),
        "w_out_c": w(ks[11], (n_odd, C_OUT, D_MODEL), C_OUT),
        "w_gate": w(ks[12], (DEPTH, D_MODEL, FFN_HIDDEN), D_MODEL),
        "w_up": w(ks[13], (DEPTH, D_MODEL, FFN_HIDDEN), D_MODEL),
        "w_down": w(ks[14], (DEPTH, FFN_HIDDEN, D_MODEL), FFN_HIDDEN),
    }


def reference(x_prompt, x_sample, g_mix_pre, g_mix_post, g_ffn_pre, g_ffn_post, w_in_ab,
              w_out_ab, rpb_a, sink_b, w_in_c, w_out_c, w_gate, w_up, w_down):
    y_prompt = trunk(x_prompt, g_mix_pre, g_mix_post, g_ffn_pre, g_ffn_post, w_in_ab, w_out_ab,
                     rpb_a, sink_b, w_in_c, w_out_c, w_gate, w_up, w_down)
    y_sample = trunk(x_sample, g_mix_pre, g_mix_post, g_ffn_pre, g_ffn_post, w_in_ab, w_out_ab,
                     rpb_a, sink_b, w_in_c, w_out_c, w_gate, w_up, w_down)
    return (y_prompt, y_sample)
```

```python
import functools
import math

import numpy as np
import jax
import jax.numpy as jnp
from jax import lax
from jax.experimental import pallas as pl
from jax.experimental.pallas import tpu as pltpu

BF16 = jnp.bfloat16
F32 = jnp.float32

LANES = 128
HEAD_DIM = 64
GRID_W = 64
NA_WIN_ROWS = 8
NA_WIN_COLS = 16
NA_QROWS = 4
NA_KROWS = 12
SW_RADIUS = 128
DIL_PAIRS = ((128, 1), (512, 4), (2048, 16))
ROPE_THETA = 500000.0
ROPE_FRACTION = 4
RMS_EPS = 1e-6
NEG_INF = -1e30
VMEM_LIMIT = 56 * 1024 * 1024


def _params(sem):
    return pltpu.CompilerParams(dimension_semantics=sem, vmem_limit_bytes=VMEM_LIMIT)


def _pick(n, candidates):
    for c in candidates:
        if n % c == 0:
            return c
    raise ValueError(f"no tile in {candidates} divides {n}")


def _rms(x, g):
    ms = jnp.mean(x * x, axis=-1, keepdims=True)
    return x * lax.rsqrt(ms + RMS_EPS) * g


def _norm_proj_body(x_ref, g_ref, w_ref, cos_ref, sa_ref, sb_ref, o_ref, xn_ref, *ys,
                    bm, bn, dil, rot_lo, rot_hi):
    j = pl.program_id(1)

    @pl.when(j == 0)
    def _():
        xn_ref[...] = _rms(x_ref[...], g_ref[...]).astype(BF16)

    y = jnp.dot(xn_ref[...], w_ref[...], preferred_element_type=F32)

    def emit(val):
        if dil == 1:
            o_ref[...] = val.astype(BF16)
        else:
            ys_ref, = ys
            for c in range(bn // LANES):
                ys_ref[c] = val[:, c * LANES:(c + 1) * LANES]
            for r in range(dil):
                for c in range(bn // LANES):
                    o_ref[0, r, :, c * LANES:(c + 1) * LANES] = (
                        ys_ref[c, pl.ds(r, bm // dil, stride=dil), :].astype(BF16))

    col0 = j * bn
    tile_rot = jnp.logical_and(col0 < rot_hi, col0 + bn > rot_lo)

    @pl.when(tile_rot)
    def _():
        cos, sa, sb = cos_ref[...], sa_ref[...], sb_ref[...]
        parts = []
        for c in range(bn // LANES):
            col = col0 + c * LANES
            colv = jnp.full((bm, LANES), col, jnp.int32)
            is_rot = jnp.logical_and(colv >= rot_lo, colv < rot_hi)
            yc = y[:, c * LANES:(c + 1) * LANES]
            rot = (yc * cos + pltpu.roll(yc, LANES - 8, 1) * sa + pltpu.roll(yc, 8, 1) * sb)
            parts.append(jnp.where(is_rot, rot, yc))
        emit(jnp.concatenate(parts, axis=1))

    @pl.when(jnp.logical_not(tile_rot))
    def _():
        emit(y)


def _norm_proj(x, g, w, tables, *, seq_len, dil, rot_lo, rot_hi):
    t, dm = x.shape
    n = w.shape[1]
    bm = _pick(seq_len, (1024, 512, 256))
    bn = _pick(n, (512, 384, 256, 128))
    blocks_per_seq = seq_len // bm
    cos, sa, sb = tables
    tab_spec = pl.BlockSpec((bm, LANES), lambda i, j: (i % blocks_per_seq, 0))
    in_specs = [
        pl.BlockSpec((bm, dm), lambda i, j: (i, 0)),
        pl.BlockSpec((1, dm), lambda i, j: (0, 0)),
        pl.BlockSpec((dm, bn), lambda i, j: (0, j)),
        tab_spec, tab_spec, tab_spec,
    ]
    scratch = [pltpu.VMEM((bm, dm), BF16)]
    if dil == 1:
        out_shape = jax.ShapeDtypeStruct((t, n), BF16)
        out_spec = pl.BlockSpec((bm, bn), lambda i, j: (i, j))
    else:
        out_shape = jax.ShapeDtypeStruct((t // seq_len, dil, seq_len // dil, n), BF16)
        out_spec = pl.BlockSpec((1, dil, bm // dil, bn),
                                lambda i, j: (i // blocks_per_seq, 0, i % blocks_per_seq, j))
        scratch.append(pltpu.VMEM((bn // LANES, bm, LANES), F32))
    body = functools.partial(_norm_proj_body, bm=bm, bn=bn, dil=dil, rot_lo=rot_lo, rot_hi=rot_hi)
    return pl.pallas_call(
        body,
        grid=(t // bm, n // bn),
        in_specs=in_specs,
        out_specs=out_spec,
        out_shape=out_shape,
        scratch_shapes=scratch,
        compiler_params=_params(("parallel", "arbitrary")),
        name=f"norm_proj_d{dil}",
    )(x, g, w, cos, sa, sb)


def _na_body(q_ref, k_ref, v_ref, bias_ref, o_ref, *, n_row_blocks, rows):
    rb = pl.program_id(2)
    variant = jnp.where(rb == 0, 0, jnp.where(rb == n_row_blocks - 1, 2, 1))
    kr0 = jnp.clip(NA_QROWS * rb - NA_WIN_ROWS // 2, 0, rows - NA_KROWS)
    ks = pl.multiple_of(kr0 * GRID_W, NA_QROWS * GRID_W)
    nq = NA_QROWS * GRID_W
    nk = NA_KROWS * GRID_W
    q2 = q_ref[0] * jnp.asarray(HEAD_DIM ** -0.5, BF16)
    k2 = k_ref[0, pl.ds(ks, nk), :]
    v2 = v_ref[0, pl.ds(ks, nk), :]
    low = lax.broadcasted_iota(jnp.int32, (nq, LANES), 1) < HEAD_DIM
    outs = []
    for e in range(2):
        qm = jnp.where(low if e == 0 else jnp.logical_not(low), q2, jnp.zeros_like(q2))
        s = lax.dot_general(qm, k2, (((1,), (1,)), ((), ())), preferred_element_type=F32)
        s = s + bias_ref[variant, e]
        m = jnp.max(s, axis=-1, keepdims=True)
        p = jnp.exp(s - m)
        den = jnp.sum(p, axis=-1, keepdims=True)
        o = jnp.dot(p.astype(BF16), v2, preferred_element_type=F32)
        outs.append(o / den)
    o_ref[0] = jnp.where(low, outs[0], outs[1]).astype(BF16)


def _na_bias(rpb, rows):
    n_row_blocks = rows // NA_QROWS
    dr_all, dc_all, ok_all = [], [], []
    for rb in (0, 1, n_row_blocks - 1):
        qrow = NA_QROWS * rb + np.arange(NA_QROWS)
        qcol = np.arange(GRID_W)
        kr0 = int(np.clip(NA_QROWS * rb - NA_WIN_ROWS // 2, 0, rows - NA_KROWS))
        krow = kr0 + np.arange(NA_KROWS)
        kcol = np.arange(GRID_W)
        rs = np.clip(qrow - NA_WIN_ROWS // 2, 0, rows - NA_WIN_ROWS)
        cs = np.clip(qcol - NA_WIN_COLS // 2, 0, GRID_W - NA_WIN_COLS)
        row_ok = (krow[None, :] >= rs[:, None]) & (krow[None, :] < rs[:, None] + NA_WIN_ROWS)
        col_ok = (kcol[None, :] >= cs[:, None]) & (kcol[None, :] < cs[:, None] + NA_WIN_COLS)
        dr = np.clip(krow[None, :] - qrow[:, None] + NA_WIN_ROWS - 1, 0, 2 * NA_WIN_ROWS - 2)
        dc = np.clip(kcol[None, :] - qcol[:, None] + NA_WIN_COLS - 1, 0, 2 * NA_WIN_COLS - 2)
        shape = (NA_QROWS, GRID_W, NA_KROWS, GRID_W)
        dr_all.append(np.broadcast_to(dr[:, None, :, None], shape))
        dc_all.append(np.broadcast_to(dc[None, :, None, :], shape))
        ok_all.append(np.broadcast_to(row_ok[:, None, :, None] & col_ok[None, :, None, :], shape))
    nq, nk = NA_QROWS * GRID_W, NA_KROWS * GRID_W
    dr = np.stack(dr_all).reshape(3, nq, nk)
    dc = np.stack(dc_all).reshape(3, nq, nk)
    ok = np.stack(ok_all).reshape(3, nq, nk)
    bias = rpb.astype(F32)[:, dr, dc]
    bias = jnp.where(ok[None], bias, NEG_INF)
    return jnp.swapaxes(bias, 0, 1)


def _neighbourhood_attention(qkv, bias, *, nseq, seq_len, width):
    rows = seq_len // GRID_W
    n_row_blocks = rows // NA_QROWS
    pairs = width // LANES
    nq, nk = NA_QROWS * GRID_W, NA_KROWS * GRID_W
    body = functools.partial(_na_body, n_row_blocks=n_row_blocks, rows=rows)
    return pl.pallas_call(
        body,
        grid=(pairs, nseq, n_row_blocks),
        in_specs=[
            pl.BlockSpec((1, nq, LANES), lambda p, n, r: (n, r, p)),
            pl.BlockSpec((1, seq_len, LANES), lambda p, n, r: (n, 0, pairs + p)),
            pl.BlockSpec((1, seq_len, LANES), lambda p, n, r: (n, 0, 2 * pairs + p)),
            pl.BlockSpec((3, 2, nq, nk), lambda p, n, r: (0, p, 0, 0)),
        ],
        out_specs=pl.BlockSpec((1, nq, LANES), lambda p, n, r: (n, r, p)),
        out_shape=jax.ShapeDtypeStruct((nseq, seq_len, width), BF16),
        compiler_params=_params(("parallel", "parallel", "arbitrary")),
        name="neighbourhood_attention",
    )(qkv, qkv, qkv, bias)


def _band_body(*refs, radius, dil, bt, bq, nk, lm, gqa, with_lse):
    refs = list(refs)
    sink_ref = refs.pop(0) if gqa else None
    q_ref, k_ref, v_ref, o_ref = refs[:4]
    refs = refs[4:]
    lse_ref = refs.pop(0) if with_lse else None
    oscr = refs.pop(0) if dil > 1 else None

    pair = pl.program_id(1)
    tb = pl.program_id(2)
    rows = bt // dil
    m0 = tb * rows
    lane_half = lax.broadcasted_iota(jnp.int32, (bq, LANES), 1) // HEAD_DIM
    low = lane_half == 0
    if gqa:
        kv_half = jnp.full((bq, LANES), (pair // 2) % 2, jnp.int32)

    for r in range(dil):
        for sb in range(rows // bq):
            qs = sb * bq
            mq0 = m0 + qs
            ks = pl.multiple_of(jnp.clip(mq0 - radius, 0, lm - nk), 16)
            q2 = q_ref[0, r, qs:qs + bq, :] * jnp.asarray(HEAD_DIM ** -0.5, BF16)
            k2 = k_ref[0, r, pl.ds(ks, nk), :]
            v2 = v_ref[0, r, pl.ds(ks, nk), :]
            qpos = mq0 + lax.broadcasted_iota(jnp.int32, (bq, nk), 0)
            kpos = ks + lax.broadcasted_iota(jnp.int32, (bq, nk), 1)
            band = jnp.abs(qpos - kpos) <= radius
            if gqa:
                q_swapped = pltpu.roll(q2.astype(F32), HEAD_DIM, 1).astype(BF16)
                kv_mask = lane_half == kv_half
            outs, lses = [], []
            for e in range(2):
                if gqa:
                    aligned = kv_half == e
                    qe = jnp.where(kv_mask, jnp.where(aligned, q2, q_swapped), jnp.zeros_like(q2))
                else:
                    qe = jnp.where(low if e == 0 else jnp.logical_not(low), q2, jnp.zeros_like(q2))
                s = lax.dot_general(qe, k2, (((1,), (1,)), ((), ())), preferred_element_type=F32)
                s = jnp.where(band, s, NEG_INF)
                m = jnp.max(s, axis=-1, keepdims=True)
                if gqa:
                    sink = sink_ref[2 * pair + e]
                    m = jnp.maximum(m, sink)
                p = jnp.exp(s - m)
                den = jnp.sum(p, axis=-1, keepdims=True)
                if gqa:
                    den = den + jnp.exp(sink - m)
                o = jnp.dot(p.astype(BF16), v2, preferred_element_type=F32) / den
                if gqa:
                    o = jnp.where(aligned, o, pltpu.roll(o, HEAD_DIM, 1))
                outs.append(o)
                if with_lse:
                    lses.append(m + jnp.log(den))
            o_nat = jnp.where(low, outs[0], outs[1])
            if dil == 1:
                o_ref[0, qs:qs + bq, :] = o_nat.astype(BF16)
                if with_lse:
                    lse_ref[0, qs:qs + bq, :] = jnp.where(low, lses[0], lses[1])
            else:
                oscr[pl.ds(qs * dil + r, bq, stride=dil), :] = o_nat
                if with_lse:
                    lse_ref[0, pl.ds(qs * dil + r, bq, stride=dil), :] = jnp.where(low, lses[0], lses[1])
    if dil > 1:
        o_ref[0] = oscr[...].astype(BF16)


def _banded_attention(qkv, sink, *, nseq, seq_len, dil, radius, width, q_col, k_col, v_col,
                      gqa, with_lse, bt, bq, nk):
    lm = seq_len // dil
    pairs = width // LANES
    rows = bt // dil
    if gqa:
        kv_blk = lambda p: p // 4
    else:
        kv_blk = lambda p: p
    in_specs = []
    args = []
    if gqa:
        in_specs.append(pl.BlockSpec(memory_space=pltpu.SMEM))
        args.append(sink)
    in_specs += [
        pl.BlockSpec((1, dil, rows, LANES), lambda n, p, t: (n, 0, t, q_col + p)),
        pl.BlockSpec((1, dil, lm, LANES), lambda n, p, t: (n, 0, 0, k_col + kv_blk(p))),
        pl.BlockSpec((1, dil, lm, LANES), lambda n, p, t: (n, 0, 0, v_col + kv_blk(p))),
    ]
    args += [qkv, qkv, qkv]
    o_spec = pl.BlockSpec((1, bt, LANES), lambda n, p, t: (n, t, p))
    out_specs = [o_spec]
    out_shape = [jax.ShapeDtypeStruct((nseq, seq_len, width), BF16)]
    if with_lse:
        out_specs.append(o_spec)
        out_shape.append(jax.ShapeDtypeStruct((nseq, seq_len, width), F32))
    scratch = [pltpu.VMEM((bt, LANES), F32)] if dil > 1 else []
    body = functools.partial(_band_body, radius=radius, dil=dil, bt=bt, bq=bq, nk=nk, lm=lm,
                             gqa=gqa, with_lse=with_lse)
    res = pl.pallas_call(
        body,
        grid=(nseq, pairs, seq_len // bt),
        in_specs=in_specs,
        out_specs=out_specs,
        out_shape=out_shape,
        scratch_shapes=scratch,
        compiler_params=_params(("parallel", "parallel", "arbitrary")),
        name=f"banded_attention_d{dil}" + ("_gqa" if gqa else ""),
    )(*args)
    return res if with_lse else res[0]


def _out_proj_ab_body(oa_ref, ob_ref, wa_ref, wb_ref, g_ref, x_ref, y_ref):
    m = jnp.dot(oa_ref[...], wa_ref[...], preferred_element_type=F32)
    m = m + jnp.dot(ob_ref[...], wb_ref[...], preferred_element_type=F32)
    y_ref[...] = x_ref[...] + _rms(m, g_ref[...])


def _out_proj_ab(oa, ob, w, g, x):
    t, dm = x.shape
    wa_rows, wb_rows = oa.shape[1], ob.shape[1]
    assert wa_rows == wb_rows and w.shape == (wa_rows + wb_rows, dm)
    bm = _pick(t, (512, 256))
    return pl.pallas_call(
        _out_proj_ab_body,
        grid=(t // bm,),
        in_specs=[
            pl.BlockSpec((bm, wa_rows), lambda i: (i, 0)),
            pl.BlockSpec((bm, wb_rows), lambda i: (i, 0)),
            pl.BlockSpec((wa_rows, dm), lambda i: (0, 0)),
            pl.BlockSpec((wb_rows, dm), lambda i: (1, 0)),
            pl.BlockSpec((1, dm), lambda i: (0, 0)),
            pl.BlockSpec((bm, dm), lambda i: (i, 0)),
        ],
        out_specs=pl.BlockSpec((bm, dm), lambda i: (i, 0)),
        out_shape=jax.ShapeDtypeStruct((t, dm), F32),
        input_output_aliases={5: 0},
        compiler_params=_params(("parallel",)),
        name="out_proj_ab",
    )(oa, ob, w, w, g, x)


def _out_proj_c_body(o0_ref, o1_ref, o2_ref, l0_ref, l1_ref, l2_ref, w_ref, g_ref, x_ref, y_ref):
    l0, l1, l2 = l0_ref[...], l1_ref[...], l2_ref[...]
    top = jnp.maximum(jnp.maximum(l0, l1), l2)
    e0, e1, e2 = jnp.exp(l0 - top), jnp.exp(l1 - top), jnp.exp(l2 - top)
    mixed = (e0 * o0_ref[...].astype(F32) + e1 * o1_ref[...].astype(F32)
             + e2 * o2_ref[...].astype(F32)) / (e0 + e1 + e2)
    m = jnp.dot(mixed.astype(BF16), w_ref[...], preferred_element_type=F32)
    y_ref[...] = x_ref[...] + _rms(m, g_ref[...])


def _out_proj_c(outs, lses, w, g, x):
    t, dm = x.shape
    cw = w.shape[0]
    bm = _pick(t, (256,))
    o_spec = pl.BlockSpec((bm, cw), lambda i: (i, 0))
    return pl.pallas_call(
        _out_proj_c_body,
        grid=(t // bm,),
        in_specs=[o_spec] * 6 + [
            pl.BlockSpec((cw, dm), lambda i: (0, 0)),
            pl.BlockSpec((1, dm), lambda i: (0, 0)),
            pl.BlockSpec((bm, dm), lambda i: (i, 0)),
        ],
        out_specs=pl.BlockSpec((bm, dm), lambda i: (i, 0)),
        out_shape=jax.ShapeDtypeStruct((t, dm), F32),
        input_output_aliases={8: 0},
        compiler_params=_params(("parallel",)),
        name="out_proj_c",
    )(*outs, *lses, w, g, x)


def _ffn_body(x_ref, gpre_ref, wg_ref, wu_ref, wd_ref, gpost_ref, y_ref, xn_ref, acc_ref, *, steps):
    f = pl.program_id(1)

    @pl.when(f == 0)
    def _():
        xn_ref[...] = _rms(x_ref[...], gpre_ref[...]).astype(BF16)
        acc_ref[...] = jnp.zeros_like(acc_ref)

    xn = xn_ref[...]
    gate = jnp.dot(xn, wg_ref[...], preferred_element_type=F32)
    up = jnp.dot(xn, wu_ref[...], preferred_element_type=F32)
    hid = gate * (1.0 / (1.0 + jnp.exp(-gate))) * up
    acc_ref[...] += jnp.dot(hid.astype(BF16), wd_ref[...], preferred_element_type=F32)

    @pl.when(f == steps - 1)
    def _():
        y_ref[...] = x_ref[...] + _rms(acc_ref[...], gpost_ref[...])


def _ffn(x, gpre, wg, wu, wd, gpost):
    t, dm = x.shape
    hidden = wg.shape[1]
    bm = _pick(t, (512, 256))
    bf = _pick(hidden, (512, 256, 128))
    steps = hidden // bf
    return pl.pallas_call(
        functools.partial(_ffn_body, steps=steps),
        grid=(t // bm, steps),
        in_specs=[
            pl.BlockSpec((bm, dm), lambda i, f: (i, 0)),
            pl.BlockSpec((1, dm), lambda i, f: (0, 0)),
            pl.BlockSpec((dm, bf), lambda i, f: (0, f)),
            pl.BlockSpec((dm, bf), lambda i, f: (0, f)),
            pl.BlockSpec((bf, dm), lambda i, f: (f, 0)),
            pl.BlockSpec((1, dm), lambda i, f: (0, 0)),
        ],
        out_specs=pl.BlockSpec((bm, dm), lambda i, f: (i, 0)),
        out_shape=jax.ShapeDtypeStruct((t, dm), F32),
        scratch_shapes=[pltpu.VMEM((bm, dm), BF16), pltpu.VMEM((bm, dm), F32)],
        input_output_aliases={0: 0},
        compiler_params=_params(("parallel", "arbitrary")),
        name="ffn",
    )(x, gpre, wg, wu, wd, gpost)


def _rotary_tables(seq_len):
    rot = HEAD_DIM // ROPE_FRACTION
    half = rot // 2
    inv = jnp.exp(-math.log(ROPE_THETA) * jnp.arange(half, dtype=F32) * (2.0 / rot))
    ang = jnp.arange(seq_len, dtype=F32)[:, None] * inv[None, :]
    cos, sin = jnp.cos(ang), jnp.sin(ang)
    lane = np.arange(LANES) % HEAD_DIM
    first = jnp.asarray(lane < half)[None, :]
    second = jnp.asarray((lane >= half) & (lane < rot))[None, :]
    idx = lane % half
    cos_l, sin_l = cos[:, idx], sin[:, idx]
    cos_t = jnp.where(first | second, cos_l, 1.0)
    sa = jnp.where(first, -sin_l, 0.0)
    sb = jnp.where(second, sin_l, 0.0)
    return cos_t, sa, sb


def _trunk(x, nseq, seq_len, g_mix_pre, g_mix_post, g_ffn_pre, g_ffn_post, w_in_ab, w_out_ab, rpb_a,
           sink_b, w_in_c, w_out_c, w_gate, w_up, w_down):
    depth = g_mix_pre.shape[0]
    a_w = rpb_a.shape[1] * HEAD_DIM
    b_qw = sink_b.shape[1] * HEAD_DIM
    b_kvw = (w_in_ab.shape[2] - 3 * a_w - b_qw) // 2
    c_w = w_out_c.shape[1]
    rows = seq_len // GRID_W
    tables = _rotary_tables(seq_len)
    row = lambda v: v.reshape(1, -1)

    for layer in range(depth):
        i = layer // 2
        if layer % 2 == 0:
            o3 = 3 * a_w
            qkv = _norm_proj(x, row(g_mix_pre[layer]), w_in_ab[i].astype(BF16), tables,
                             seq_len=seq_len, dil=1, rot_lo=o3, rot_hi=o3 + b_qw + b_kvw)
            qkv3 = qkv.reshape(nseq, seq_len, -1)
            out_a = _neighbourhood_attention(qkv3, _na_bias(rpb_a[i], rows),
                                             nseq=nseq, seq_len=seq_len, width=a_w)
            out_b = _banded_attention(
                qkv3.reshape(nseq, 1, seq_len, -1), sink_b[i].astype(F32),
                nseq=nseq, seq_len=seq_len, dil=1, radius=SW_RADIUS, width=b_qw,
                q_col=o3 // LANES, k_col=(o3 + b_qw) // LANES, v_col=(o3 + b_qw + b_kvw) // LANES,
                gqa=True, with_lse=False, bt=1024, bq=256, nk=256 + 2 * SW_RADIUS)
            x = _out_proj_ab(out_a.reshape(-1, a_w), out_b.reshape(-1, b_qw),
                             w_out_ab[i].astype(BF16), row(g_mix_post[layer]), x)
        else:
            outs, lses = [], []
            for gi, (win, dil) in enumerate(DIL_PAIRS):
                radius = win // (2 * dil)
                w_g = w_in_c[i][:, gi * 3 * c_w:(gi + 1) * 3 * c_w].astype(BF16)
                qkv = _norm_proj(x, row(g_mix_pre[layer]), w_g, tables,
                                 seq_len=seq_len, dil=dil, rot_lo=0, rot_hi=2 * c_w)
                qkv = qkv.reshape(nseq, dil, seq_len // dil, 3 * c_w)
                lm = seq_len // dil
                bq = min(256, lm)
                nk = min(bq + 2 * radius, lm)
                o, lse = _banded_attention(
                    qkv, None, nseq=nseq, seq_len=seq_len, dil=dil, radius=radius, width=c_w,
                    q_col=0, k_col=c_w // LANES, v_col=2 * c_w // LANES,
                    gqa=False, with_lse=True, bt=max(1024, bq * dil), bq=bq, nk=nk)
                outs.append(o.reshape(-1, c_w))
                lses.append(lse.reshape(-1, c_w))
            x = _out_proj_c(outs, lses, w_out_c[i].astype(BF16), row(g_mix_post[layer]), x)
        x = _ffn(x, row(g_ffn_pre[layer]), w_gate[layer].astype(BF16), w_up[layer].astype(BF16),
                 w_down[layer].astype(BF16), row(g_ffn_post[layer]))
    return x


def kernel(x_prompt, x_sample, g_mix_pre, g_mix_post, g_ffn_pre, g_ffn_post, w_in_ab, w_out_ab, rpb_a,
           sink_b, w_in_c, w_out_c, w_gate, w_up, w_down):
    nb, seq_len, dm = x_prompt.shape
    assert x_sample.shape[1:] == (seq_len, dm)
    nseq = nb + x_sample.shape[0]
    x = jnp.concatenate([x_prompt, x_sample], axis=0).reshape(nseq * seq_len, dm)
    y = _trunk(x, nseq, seq_len, g_mix_pre, g_mix_post, g_ffn_pre, g_ffn_post, w_in_ab, w_out_ab,
               rpb_a, sink_b, w_in_c, w_out_c, w_gate, w_up, w_down)
    y = y.reshape(nseq, seq_len, dm)
    return y[:nb], y[nb:]
```
